```python
import math
import jax, jax.numpy as jnp
from jax import lax
import numpy as np

D_MODEL = 1024
BATCH = 1
SEQ = 16384
DEPTH = 1

ATTN_GROUPS = ((128, 1), (512, 4), (2048, 16))
N_GROUPS = len(ATTN_GROUPS)
HEADS_PER_GROUP = 8
HEAD_DIM = 64
ATTN_HEADS = N_GROUPS * HEADS_PER_GROUP
ATTN_WIDTH = ATTN_HEADS * HEAD_DIM
ATTN_OUT = HEADS_PER_GROUP * HEAD_DIM
ATTN_BLOCK = 64
REL_BUCKETS = 32
REL_MAX_DIST = 1024
HGRN_EXPAND = 128
HGRN_HEADS = D_MODEL // HGRN_EXPAND
HGRN_KDIM = HGRN_EXPAND
HGRN_VDIM = D_MODEL // HGRN_HEADS
HGRN_WIDTH = HGRN_HEADS * HGRN_KDIM
HGRN_CHUNK = 64
D_FF = 2816
N_BRANCHES = 2
EPS = 1e-6
NEG_INF = -1e30

IN_SPLITS = (ATTN_WIDTH,) * 3 + (HGRN_WIDTH,) * 5 + (D_MODEL,) * N_BRANCHES
IN_WIDTH = sum(IN_SPLITS)
IN_OFFSETS = tuple(int(v) for v in np.cumsum(IN_SPLITS)[:-1])

kernel_name = 'hybrid_dilated_attn_hgrn2_macaron_block'


def rmsnorm(x, g):
    xf = x.astype(jnp.float32)
    y = xf * lax.rsqrt(jnp.mean(xf * xf, axis=-1, keepdims=True) + EPS)
    return (y * g.astype(jnp.float32)).astype(x.dtype)


def swiglu(x, w_in, w_out):
    gate, up = jnp.split(x @ w_in, 2, axis=-1)
    return (jax.nn.silu(gate) * up) @ w_out


def t5_bucket(rel):
    nb = REL_BUCKETS // 2
    max_exact = nb // 2
    ret = jnp.where(rel > 0, nb, 0)
    n = jnp.abs(rel)
    nf = jnp.maximum(n, 1).astype(jnp.float32)
    large = max_exact + (jnp.log(nf / max_exact) / math.log(REL_MAX_DIST / max_exact)
                         * (nb - max_exact)).astype(jnp.int32)
    large = jnp.minimum(large, nb - 1)
    return ret + jnp.where(n < max_exact, n, large)


def dilated_group_attention(q, k, v, bias_table, window, dilation):
    B, S, Hg, hd = q.shape
    half = window // (2 * dilation)
    L = S // dilation
    nb = -(-L // ATTN_BLOCK)
    Lp = nb * ATTN_BLOCK

    def to_sub(t):
        return t.reshape(B, L, dilation, Hg, hd).transpose(0, 2, 3, 1, 4)

    qs = jnp.pad(to_sub(q), ((0, 0),) * 3 + ((0, Lp - L), (0, 0)))
    pad_kv = ((0, 0),) * 3 + ((ATTN_BLOCK, Lp - L + ATTN_BLOCK), (0, 0))
    ks = jnp.pad(to_sub(k), pad_kv)
    vs = jnp.pad(to_sub(v), pad_kv)
    qb = qs.reshape(B, dilation, Hg, nb, ATTN_BLOCK, hd)

    def band(t):
        tb = t.reshape(B, dilation, Hg, nb + 2, ATTN_BLOCK, hd)
        return jnp.concatenate([tb[:, :, :, :nb], tb[:, :, :, 1:nb + 1], tb[:, :, :, 2:]], axis=-2)

    kb, vb = band(ks), band(vs)
    t_idx = jnp.arange(ATTN_BLOCK)[:, None]
    c_idx = jnp.arange(3 * ATTN_BLOCK)[None, :]
    delta = c_idx - ATTN_BLOCK - t_idx
    key_sub = jnp.arange(nb)[:, None, None] * ATTN_BLOCK + c_idx[None] - ATTN_BLOCK
    valid = (jnp.abs(delta) <= half)[None] & (key_sub >= 0) & (key_sub < L)
    bias = jnp.moveaxis(bias_table.astype(jnp.float32)[t5_bucket(delta * dilation)], -1, 0)

    s = jnp.einsum('bdhnqe,bdhnke->bdhnqk', qb, kb).astype(jnp.float32) * (HEAD_DIM ** -0.5)
    s = s + bias[None, None, :, None]
    s = jnp.where(valid[None, None, None], s, NEG_INF)
    lse = jax.nn.logsumexp(s, axis=-1)
    p = jnp.exp(s - lse[..., None])
    o = jnp.einsum('bdhnqk,bdhnke->bdhnqe', p.astype(v.dtype), vb)
    o = o.reshape(B, dilation, Hg, Lp, hd)[:, :, :, :L]
    lse = lse.reshape(B, dilation, Hg, Lp)[..., :L]
    o = o.transpose(0, 3, 1, 2, 4).reshape(B, S, Hg, hd)
    lse = lse.transpose(0, 3, 1, 2).reshape(B, S, Hg)
    return o, lse


def hgrn2_chunk_scan(q, k, v, logf):
    B, S, H, K = q.shape
    V = v.shape[-1]
    C = HGRN_CHUNK
    N = S // C

    def r(t):
        return t.reshape(B, N, C, H, t.shape[-1]).transpose(0, 3, 1, 2, 4)

    q, k, v, logf = r(q), r(k), r(v), r(logf)
    b = jnp.cumsum(logf, axis=3)
    ref = b[:, :, :, C // 2:C // 2 + 1]
    a = jnp.einsum('bhnck,bhnsk->bhncs', q * jnp.exp(b - ref), k * jnp.exp(ref - b))
    tri = jnp.tril(jnp.ones((C, C), dtype=bool))
    a = jnp.where(tri, a, 0.0)
    o_intra = jnp.einsum('bhncs,bhnsv->bhncv', a, v)
    b_last = b[:, :, :, -1:]
    u = jnp.einsum('bhnck,bhncv->bhnkv', k * jnp.exp(b_last - b), v)
    decay = jnp.exp(b_last[:, :, :, 0])

    def step(state, inp):
        d, du = inp
        return d[..., None] * state + du, state

    s0 = jnp.zeros((B, H, K, V), jnp.float32)
    _, s_start = lax.scan(step, s0, (decay.transpose(2, 0, 1, 3), u.transpose(2, 0, 1, 3, 4)))
    s_start = s_start.transpose(1, 2, 0, 3, 4)
    o_inter = jnp.einsum('bhnck,bhnkv->bhncv', q * jnp.exp(b), s_start)
    o = o_intra + o_inter
    return o.transpose(0, 2, 3, 1, 4).reshape(B, S, H, V)


def hgrn2_bidirectional(qz, fz_f, fz_b, iz, gz, lb_f, lb_b, norm_w):
    B, S, _ = qz.shape
    shp_k = (B, S, HGRN_HEADS, HGRN_KDIM)
    q = jax.nn.silu(qz.astype(jnp.float32)).reshape(shp_k)
    v = iz.astype(jnp.float32).reshape(B, S, HGRN_HEADS, HGRN_VDIM)

    def gates(fz, lb):
        f = lb.astype(jnp.float32) + (1.0 - lb.astype(jnp.float32)) * jax.nn.sigmoid(fz.astype(jnp.float32))
        return (1.0 - f).reshape(shp_k), jnp.log(f).reshape(shp_k)

    k_f, logf_f = gates(fz_f, lb_f)
    k_b, logf_b = gates(fz_b, lb_b)
    o_fwd = hgrn2_chunk_scan(q, k_f, v, logf_f)
    flip = lambda t: jnp.flip(t, axis=1)
    o_bwd = flip(hgrn2_chunk_scan(flip(q), flip(k_b), flip(v), flip(logf_b)))
    o = o_fwd + o_bwd
    o = o * lax.rsqrt(jnp.mean(o * o, axis=-1, keepdims=True) + EPS) * norm_w.astype(jnp.float32)
    o = o * jax.nn.silu(gz.astype(jnp.float32).reshape(B, S, HGRN_HEADS, HGRN_VDIM))
    return o.reshape(B, S, HGRN_HEADS * HGRN_VDIM).astype(qz.dtype)


def hybrid_mixer(h, w_in, rel_bias, lb_f, lb_b, hgrn_norm, w_proj_attn, w_proj_hgrn, w_out):
    B, S, _ = h.shape
    z = h @ w_in
    qa, ka, va, qh, fz_f, fz_b, ih, gh, gate_a, gate_h = jnp.split(z, IN_OFFSETS, axis=-1)

    shp = (B, S, N_GROUPS, HEADS_PER_GROUP, HEAD_DIM)
    qa, ka, va = qa.reshape(shp), ka.reshape(shp), va.reshape(shp)
    bias = rel_bias.reshape(REL_BUCKETS, N_GROUPS, HEADS_PER_GROUP)
    outs, lses = [], []
    for g, (window, dil) in enumerate(ATTN_GROUPS):
        o, lse = dilated_group_attention(qa[:, :, g], ka[:, :, g], va[:, :, g], bias[:, g], window, dil)
        outs.append(o)
        lses.append(lse)
    alpha = jax.nn.softmax(jnp.stack(lses, axis=0), axis=0)
    y_a = jnp.einsum('gbsh,gbshe->bshe', alpha, jnp.stack(outs, axis=0).astype(jnp.float32))
    y_a = y_a.reshape(B, S, ATTN_OUT).astype(h.dtype)

    y_h = hgrn2_bidirectional(qh, fz_f, fz_b, ih, gh, lb_f, lb_b, hgrn_norm)

    y = jax.nn.sigmoid(gate_a) * (y_a @ w_proj_attn) + jax.nn.sigmoid(gate_h) * (y_h @ w_proj_hgrn)
    return y @ w_out


def setup_inputs(seed: int = 0) -> dict:
    key = jax.random.key(seed)
    ks = jax.random.split(key, 18)
    f32 = jnp.float32

    def w(k, shape, fan_in):
        return jax.random.normal(k, shape, f32) * fan_in ** -0.5

    def gain(k, shape):
        return 1.0 + 0.02 * jax.random.normal(k, shape, f32)

    L = DEPTH
    return {
        'x': jax.random.normal(ks[0], (BATCH, SEQ, D_MODEL), f32),
        'ffn1_pre_norm': gain(ks[1], (L, D_MODEL)),
        'w_ffn1_in': w(ks[2], (L, D_MODEL, 2 * D_FF), D_MODEL),
        'w_ffn1_out': w(ks[3], (L, D_FF, D_MODEL), D_FF),
        'ffn1_post_norm': gain(ks[4], (L, D_MODEL)),
        'mix_pre_norm': gain(ks[5], (L, D_MODEL)),
        'w_in': w(ks[6], (L, D_MODEL, IN_WIDTH), D_MODEL),
        'rel_bias': 0.2 * jax.random.normal(ks[7], (REL_BUCKETS, ATTN_HEADS), f32),
        'hgrn_lb': 0.5 * jax.random.normal(ks[8], (2, DEPTH + 1, HGRN_WIDTH), f32),
        'hgrn_norm': gain(ks[9], (L, HGRN_VDIM)),
        'w_proj_attn': w(ks[10], (L, ATTN_OUT, D_MODEL), ATTN_OUT),
        'w_proj_hgrn': w(ks[11], (L, HGRN_HEADS * HGRN_VDIM, D_MODEL), HGRN_HEADS * HGRN_VDIM),
        'w_out': w(ks[12], (L, D_MODEL, D_MODEL), D_MODEL),
        'mix_post_norm': gain(ks[13], (L, D_MODEL)),
        'ffn2_pre_norm': gain(ks[14], (L, D_MODEL)),
        'w_ffn2_in': w(ks[15], (L, D_MODEL, 2 * D_FF), D_MODEL),
        'w_ffn2_out': w(ks[16], (L, D_FF, D_MODEL), D_FF),
        'ffn2_post_norm': gain(ks[17], (L, D_MODEL)),
    }


def reference(x, ffn1_pre_norm, w_ffn1_in, w_ffn1_out, ffn1_post_norm, mix_pre_norm, w_in,
              rel_bias, hgrn_lb, hgrn_norm, w_proj_attn, w_proj_hgrn, w_out, mix_post_norm,
              ffn2_pre_norm, w_ffn2_in, w_ffn2_out, ffn2_post_norm):
    lb_all = jnp.cumsum(jax.nn.softmax(hgrn_lb.astype(jnp.float32), axis=1), axis=1)
    for l in range(DEPTH):
        h = rmsnorm(x, ffn1_pre_norm[l])
        x = x + 0.5 * rmsnorm(swiglu(h, w_ffn1_in[l], w_ffn1_out[l]), ffn1_post_norm[l])

        h = rmsnorm(x, mix_pre_norm[l])
        y = hybrid_mixer(h, w_in[l], rel_bias, lb_all[0, l], lb_all[1, l], hgrn_norm[l],
                         w_proj_attn[l], w_proj_hgrn[l], w_out[l])
        x = x + rmsnorm(y, mix_post_norm[l])

        h = rmsnorm(x, ffn2_pre_norm[l])
        x = x + 0.5 * rmsnorm(swiglu(h, w_ffn2_in[l], w_ffn2_out[l]), ffn2_post_norm[l])
    return x
```

```python
import functools
import math

import jax
import jax.numpy as jnp
import numpy as np
from jax import lax
from jax.experimental import pallas as pl
from jax.experimental.pallas import tpu as pltpu

F32 = jnp.float32
BF16 = jnp.bfloat16

D_MODEL = 1024
ATTN_GROUPS = ((128, 1), (512, 4), (2048, 16))
N_GROUPS = len(ATTN_GROUPS)
HEADS_PER_GROUP = 8
HEAD_DIM = 64
GROUP_WIDTH = HEADS_PER_GROUP * HEAD_DIM
ATTN_WIDTH = N_GROUPS * GROUP_WIDTH
REL_BUCKETS = 32
REL_MAX_DIST = 1024
HGRN_HEADS = 8
HGRN_DIM = 128
HGRN_WIDTH = HGRN_HEADS * HGRN_DIM
HGRN_CHUNK = 64
D_FF = 2816
EPS = 1e-6
NEG_INF = -1e30

LANES = 128
MXU_COLS = 256
VMEM_LIMIT_BYTES = 56 * 1024 * 1024

FFN_ROWS = 512
FFN_CHUNK = MXU_COLS
PROJ_ROWS = 1024
PROJ_COLS = 512
ATTN_ROWS = 512
ATTN_SUB = 128
ATTN_HALO = 64
SCAN_ROWS = 256
SCAN_LANES = 2 * HGRN_DIM
MIX_ROWS = 512
LSE_LANES = LANES


def _rms(x, g):
    return x * lax.rsqrt(jnp.mean(x * x, axis=-1, keepdims=True) + EPS) * g


def _params(*semantics):
    return pltpu.CompilerParams(dimension_semantics=semantics, vmem_limit_bytes=VMEM_LIMIT_BYTES)


def _resident(shape):
    zeros = (0,) * len(shape)
    return pl.BlockSpec(shape, lambda *_: zeros, pipeline_mode=pl.Buffered(1))


def _ffn_kernel(x_ref, pre_ref, wgu_ref, wo_ref, post_ref, *rest, emit_next):
    if emit_next:
        nxt_ref, x_out_ref, h_out_ref = rest
    else:
        (x_out_ref,) = rest
    x = x_ref[...]
    h = _rms(x, pre_ref[...]).astype(BF16)
    acc = None
    for c in range(D_FF // FFN_CHUNK):
        gu = jnp.dot(h, wgu_ref[:, c * 2 * FFN_CHUNK:(c + 1) * 2 * FFN_CHUNK],
                     preferred_element_type=F32)
        gate, up = gu[:, :FFN_CHUNK], gu[:, FFN_CHUNK:]
        act = (gate * jax.nn.sigmoid(gate) * up).astype(BF16)
        part = jnp.dot(act, wo_ref[c * FFN_CHUNK:(c + 1) * FFN_CHUNK, :], preferred_element_type=F32)
        acc = part if acc is None else acc + part
    x1 = x + 0.5 * _rms(acc, post_ref[...])
    x_out_ref[...] = x1
    if emit_next:
        h_out_ref[...] = _rms(x1, nxt_ref[...]).astype(BF16)


def _ffn(x2d, pre, wgu, wo, post, nxt):
    s = x2d.shape[0]
    emit_next = nxt is not None
    row = lambda i: (i, 0)
    in_specs = [pl.BlockSpec((FFN_ROWS, D_MODEL), row), _resident((1, D_MODEL)),
                _resident(wgu.shape), _resident(wo.shape), _resident((1, D_MODEL))]
    args = [x2d, pre, wgu, wo, post]
    out_shape = [jax.ShapeDtypeStruct((s, D_MODEL), F32)]
    out_specs = [pl.BlockSpec((FFN_ROWS, D_MODEL), row)]
    if emit_next:
        in_specs.append(_resident((1, D_MODEL)))
        args.append(nxt)
        out_shape.append(jax.ShapeDtypeStruct((s, D_MODEL), BF16))
        out_specs.append(pl.BlockSpec((FFN_ROWS, D_MODEL), row))
    return pl.pallas_call(
        functools.partial(_ffn_kernel, emit_next=emit_next),
        grid=(s // FFN_ROWS,), in_specs=in_specs, out_specs=out_specs, out_shape=out_shape,
        compiler_params=_params("parallel"), name="ffn_next" if emit_next else "ffn",
    )(*args)


def _proj_kernel(h_ref, w_ref, *rest, epilogue):
    z = jnp.dot(h_ref[...], w_ref[...], preferred_element_type=F32)
    if epilogue == "logf":
        lb_ref, o_ref = rest
        lb = lb_ref[...]
        o_ref[...] = jnp.log(lb + (1.0 - lb) * jax.nn.sigmoid(z))
        return
    (o_ref,) = rest
    if epilogue == "silu":
        z = z * jax.nn.sigmoid(z)
    elif epilogue == "sigmoid":
        z = jax.nn.sigmoid(z)
    o_ref[...] = z.astype(o_ref.dtype)


def _proj(h, w, epilogue, out_dtype, lb=None):
    s, n = h.shape[0], w.shape[1]
    in_specs = [pl.BlockSpec((PROJ_ROWS, D_MODEL), lambda i, j: (i, 0)),
                pl.BlockSpec((D_MODEL, PROJ_COLS), lambda i, j: (0, j))]
    args = [h, w]
    if lb is not None:
        in_specs.append(pl.BlockSpec((1, PROJ_COLS), lambda i, j: (0, j)))
        args.append(lb)
    return pl.pallas_call(
        functools.partial(_proj_kernel, epilogue=epilogue),
        grid=(s // PROJ_ROWS, n // PROJ_COLS), in_specs=in_specs,
        out_specs=pl.BlockSpec((PROJ_ROWS, PROJ_COLS), lambda i, j: (i, j)),
        out_shape=jax.ShapeDtypeStruct((s, n), out_dtype),
        compiler_params=_params("parallel", "parallel"), name="proj_" + epilogue,
    )(*args)


def _attn_kernel(q_ref, kp_ref, km_ref, kn_ref, vp_ref, vm_ref, vn_ref, bias_ref,
                 o_ref, lse_ref, kcat_ref, vcat_ref):
    n, n_last = pl.program_id(1), pl.num_programs(1) - 1
    kcat_ref[0:ATTN_HALO] = kp_ref[...]
    kcat_ref[ATTN_HALO:ATTN_HALO + ATTN_ROWS] = km_ref[...]
    kcat_ref[ATTN_HALO + ATTN_ROWS:] = kn_ref[...]
    vcat_ref[0:ATTN_HALO] = vp_ref[...]
    vcat_ref[ATTN_HALO:ATTN_HALO + ATTN_ROWS] = vm_ref[...]
    vcat_ref[ATTN_HALO + ATTN_ROWS:] = vn_ref[...]

    n_keys = ATTN_SUB + 2 * ATTN_HALO
    n_sub = ATTN_ROWS // ATTN_SUB
    head_lane = lax.broadcasted_iota(jnp.int32, (1, LANES), 1) // HEAD_DIM
    key_col = lax.broadcasted_iota(jnp.int32, (1, n_keys), 1)
    out_lane = lax.broadcasted_iota(jnp.int32, (1, LSE_LANES), 1)

    def sub_block(i, carry):
        r0 = pl.multiple_of(i * ATTN_SUB, ATTN_SUB)
        q = q_ref[pl.ds(r0, ATTN_SUB), :] * (HEAD_DIM ** -0.5)
        k = kcat_ref[pl.ds(r0, n_keys), :]
        v = vcat_ref[pl.ds(r0, n_keys), :]
        before = (key_col < ATTN_HALO) & (n == 0) & (i == 0)
        after = (key_col >= ATTN_HALO + ATTN_SUB) & (n == n_last) & (i == n_sub - 1)
        edge = jnp.where(before | after, NEG_INF, 0.0).astype(F32)
        lse_tile = jnp.zeros((ATTN_SUB, LSE_LANES), F32)
        for pair in range(HEADS_PER_GROUP // 2):
            lanes = slice(pair * LANES, (pair + 1) * LANES)
            qp, kp, vp = q[:, lanes], k[:, lanes], v[:, lanes]
            o_pair = None
            for e in range(2):
                head = 2 * pair + e
                sel = head_lane == e
                s = lax.dot_general(jnp.where(sel, qp, 0), kp, (((1,), (1,)), ((), ())),
                                    preferred_element_type=F32)
                s = s + bias_ref[head] + edge
                m = jnp.max(s, axis=-1, keepdims=True)
                p = jnp.exp(s - m)
                l = jnp.sum(p, axis=-1, keepdims=True)
                acc = jnp.dot(p.astype(BF16), jnp.where(sel, vp, 0), preferred_element_type=F32)
                o_head = acc * (1.0 / l)
                o_pair = o_head if o_pair is None else o_pair + o_head
                lse_tile = jnp.where(out_lane == head, m + jnp.log(l), lse_tile)
            o_ref[pl.ds(r0, ATTN_SUB), lanes] = o_pair.astype(o_ref.dtype)
        lse_ref[pl.ds(r0, ATTN_SUB), :] = lse_tile
        return carry

    lax.fori_loop(0, n_sub, sub_block, 0)


def _attn(zc, bias, group, dilation, seq):
    sub_len = seq // dilation
    blocks_per_row = zc.shape[1] // GROUP_WIDTH
    zv = zc.reshape(sub_len, dilation * zc.shape[1])
    halo_blocks = ATTN_ROWS // ATTN_HALO
    last_halo = sub_len // ATTN_HALO - 1

    def col(r, which):
        return r * blocks_per_row + which * N_GROUPS + group

    def main(which):
        return pl.BlockSpec((ATTN_ROWS, GROUP_WIDTH), lambda r, n: (n, col(r, which)))

    def prev(which):
        return pl.BlockSpec((ATTN_HALO, GROUP_WIDTH),
                            lambda r, n: (jnp.maximum(n * halo_blocks - 1, 0), col(r, which)))

    def nxt(which):
        return pl.BlockSpec((ATTN_HALO, GROUP_WIDTH),
                            lambda r, n: (jnp.minimum((n + 1) * halo_blocks, last_halo), col(r, which)))

    o, lse = pl.pallas_call(
        _attn_kernel,
        grid=(dilation, sub_len // ATTN_ROWS),
        in_specs=[main(0), prev(1), main(1), nxt(1), prev(2), main(2), nxt(2), _resident(bias.shape)],
        out_specs=[pl.BlockSpec((ATTN_ROWS, GROUP_WIDTH), lambda r, n: (n, r)),
                   pl.BlockSpec((ATTN_ROWS, LSE_LANES), lambda r, n: (n, r))],
        out_shape=[jax.ShapeDtypeStruct((sub_len, dilation * GROUP_WIDTH), BF16),
                   jax.ShapeDtypeStruct((sub_len, dilation * LSE_LANES), F32)],
        scratch_shapes=[pltpu.VMEM((ATTN_ROWS + 2 * ATTN_HALO, GROUP_WIDTH), BF16),
                        pltpu.VMEM((ATTN_ROWS + 2 * ATTN_HALO, GROUP_WIDTH), BF16)],
        compiler_params=_params("parallel", "parallel"), name=f"attn_d{dilation}",
    )(zv, zv, zv, zv, zv, zv, zv, bias)
    return o.reshape(seq, GROUP_WIDTH), lse.reshape(seq, LSE_LANES)


def _t5_bucket(rel):
    nb = REL_BUCKETS // 2
    max_exact = nb // 2
    ret = jnp.where(rel > 0, nb, 0)
    n = jnp.abs(rel)
    nf = jnp.maximum(n, 1).astype(F32)
    large = max_exact + (jnp.log(nf / max_exact) / math.log(REL_MAX_DIST / max_exact)
                         * (nb - max_exact)).astype(jnp.int32)
    large = jnp.minimum(large, nb - 1)
    return ret + jnp.where(n < max_exact, n, large)


def _band_bias(rel_bias, group, dilation):
    t = jnp.arange(ATTN_SUB)[:, None]
    c = jnp.arange(ATTN_SUB + 2 * ATTN_HALO)[None, :]
    delta = c - ATTN_HALO - t
    table = rel_bias.astype(F32)[:, group * HEADS_PER_GROUP:(group + 1) * HEADS_PER_GROUP]
    bias = jnp.moveaxis(table[_t5_bucket(delta * dilation)], -1, 0)
    return jnp.where((jnp.abs(delta) <= ATTN_HALO)[None], bias, NEG_INF)


def _scan_kernel(q_ref, v_ref, lf_ref, tri_ref, o_ref, st_ref, *, reverse):
    chunk = HGRN_CHUNK
    n_chunks = SCAN_ROWS // chunk
    ref_row = chunk // 2 - 1 if reverse else chunk // 2
    last_row = 0 if reverse else chunk - 1

    @pl.when(pl.program_id(1) == 0)
    def _():
        st_ref[...] = jnp.zeros_like(st_ref)

    lf = lf_ref[...]
    hi = lf.astype(BF16)
    r1 = lf - hi.astype(F32)
    mid = r1.astype(BF16)
    lo = (r1 - mid.astype(F32)).astype(BF16)
    tri = tri_ref[...]
    b = (jnp.dot(tri, hi, preferred_element_type=F32) + jnp.dot(tri, mid, preferred_element_type=F32)
         + jnp.dot(tri, lo, preferred_element_type=F32))
    kk = 1.0 - jnp.exp(lf)
    q = q_ref[...].astype(F32)
    v = v_ref[...]

    row = lax.broadcasted_iota(jnp.int32, (chunk, chunk), 0)
    colm = lax.broadcasted_iota(jnp.int32, (chunk, chunk), 1)
    causal = (colm >= row) if reverse else (colm <= row)
    st_row = lax.broadcasted_iota(jnp.int32, (SCAN_LANES, SCAN_LANES), 0) // HGRN_DIM
    st_col = lax.broadcasted_iota(jnp.int32, (SCAN_LANES, SCAN_LANES), 1) // HGRN_DIM
    same_head = st_row == st_col

    order = range(n_chunks - 1, -1, -1) if reverse else range(n_chunks)
    for n in order:
        rows = slice(n * chunk, (n + 1) * chunk)
        bn = b[rows]
        ref = bn[ref_row:ref_row + 1]
        last = bn[last_row:last_row + 1]
        qe = q[rows] * jnp.exp(bn - ref)
        ke = kk[rows] * jnp.exp(ref - bn)
        qb = (qe * jnp.exp(ref)).astype(BF16)
        kd = (ke * jnp.exp(last - ref)).astype(BF16)
        qe = qe.astype(BF16)
        ke = ke.astype(BF16)
        vn = v[rows]
        o_intra = []
        for e in range(2):
            lanes = slice(e * HGRN_DIM, (e + 1) * HGRN_DIM)
            a = lax.dot_general(qe[:, lanes], ke[:, lanes], (((1,), (1,)), ((), ())),
                                preferred_element_type=F32)
            a = jnp.where(causal, a, 0.0).astype(BF16)
            o_intra.append(jnp.dot(a, vn[:, lanes], preferred_element_type=F32))
        st = st_ref[...]
        o_inter = lax.dot_general(qb, st.astype(BF16), (((1,), (1,)), ((), ())),
                                  preferred_element_type=F32)
        o_ref[rows, :] = jnp.concatenate(o_intra, axis=1) + o_inter
        ut = lax.dot_general(vn, kd, (((0,), (0,)), ((), ())), preferred_element_type=F32)
        st_ref[...] = st * jnp.exp(last) + jnp.where(same_head, ut, 0.0)


def _scan(qg, zc, logf, tri, seq, reverse):
    n_blocks = seq // SCAN_ROWS
    pairs = HGRN_WIDTH // SCAN_LANES
    v_col0 = 3 * ATTN_WIDTH // SCAN_LANES
    lf_col0 = pairs if reverse else 0
    blk = (lambda i: n_blocks - 1 - i) if reverse else (lambda i: i)
    return pl.pallas_call(
        functools.partial(_scan_kernel, reverse=reverse),
        grid=(pairs, n_blocks),
        in_specs=[pl.BlockSpec((SCAN_ROWS, SCAN_LANES), lambda p, i: (blk(i), p)),
                  pl.BlockSpec((SCAN_ROWS, SCAN_LANES), lambda p, i: (blk(i), v_col0 + p)),
                  pl.BlockSpec((SCAN_ROWS, SCAN_LANES), lambda p, i: (blk(i), lf_col0 + p)),
                  _resident(tri.shape)],
        out_specs=pl.BlockSpec((SCAN_ROWS, SCAN_LANES), lambda p, i: (blk(i), p)),
        out_shape=jax.ShapeDtypeStruct((seq, HGRN_WIDTH), F32),
        scratch_shapes=[pltpu.VMEM((SCAN_LANES, SCAN_LANES), F32)],
        compiler_params=_params("parallel", "arbitrary"), name="scan_bwd" if reverse else "scan_fwd",
    )(qg, zc, logf, tri)


def _chunk_tri(reverse):
    t = np.arange(SCAN_ROWS)
    same_chunk = (t[:, None] // HGRN_CHUNK) == (t[None, :] // HGRN_CHUNK)
    order = (t[None, :] >= t[:, None]) if reverse else (t[None, :] <= t[:, None])
    return jnp.asarray(same_chunk & order, dtype=BF16)


def _split3(a):
    hi = a.astype(BF16)
    r1 = a - hi.astype(F32)
    mid = r1.astype(BF16)
    lo = (r1 - mid.astype(F32)).astype(BF16)
    return hi, mid, lo


def _mix_out_kernel(x_ref, o0_ref, o1_ref, o2_ref, l0_ref, l1_ref, l2_ref, of_ref, ob_ref, gs_ref,
                    gates_ref, expand_ref, wpa_ref, wph_ref, wout_ref, hnorm_ref, post_ref, x_out_ref):
    ls = [l0_ref[...], l1_ref[...], l2_ref[...]]
    m = jnp.maximum(jnp.maximum(ls[0], ls[1]), ls[2])
    ws = [jnp.exp(l - m) for l in ls]
    inv = 1.0 / (ws[0] + ws[1] + ws[2])
    expand = expand_ref[...]
    ya = None
    for w, o_ref in zip(ws, (o0_ref, o1_ref, o2_ref)):
        alpha = sum(jnp.dot(part, expand, preferred_element_type=F32) for part in _split3(w * inv))
        term = alpha * o_ref[...].astype(F32)
        ya = term if ya is None else ya + term
    pa = jnp.dot(ya.astype(BF16), wpa_ref[...], preferred_element_type=F32)

    o = of_ref[...] + ob_ref[...]
    hnorm = hnorm_ref[...]
    normed = []
    for h in range(HGRN_HEADS):
        oh = o[:, h * HGRN_DIM:(h + 1) * HGRN_DIM]
        normed.append(oh * lax.rsqrt(jnp.mean(oh * oh, axis=-1, keepdims=True) + EPS) * hnorm)
    yh = jnp.concatenate(normed, axis=1) * gs_ref[...].astype(F32)
    ph = jnp.dot(yh.astype(BF16), wph_ref[...], preferred_element_type=F32)

    gates = gates_ref[...].astype(F32)
    y = gates[:, :D_MODEL] * pa + gates[:, D_MODEL:] * ph
    yo = jnp.dot(y.astype(BF16), wout_ref[...], preferred_element_type=F32)
    x_out_ref[...] = x_ref[...] + _rms(yo, post_ref[...])


def _mix_out(x1, attn_o, attn_lse, o_fwd, o_bwd, qg, gates, expand, wpa, wph, wout, hnorm, post):
    s = x1.shape[0]
    row = lambda i: (i, 0)
    rows = lambda width: pl.BlockSpec((MIX_ROWS, width), row)
    in_specs = ([rows(D_MODEL)] + [rows(GROUP_WIDTH)] * 3 + [rows(LSE_LANES)] * 3
                + [rows(HGRN_WIDTH)] * 2
                + [pl.BlockSpec((MIX_ROWS, HGRN_WIDTH), lambda i: (i, 1)), rows(2 * D_MODEL)]
                + [_resident(a.shape) for a in (expand, wpa, wph, wout, hnorm, post)])
    return pl.pallas_call(
        _mix_out_kernel, grid=(s // MIX_ROWS,), in_specs=in_specs,
        out_specs=rows(D_MODEL), out_shape=jax.ShapeDtypeStruct((s, D_MODEL), F32),
        compiler_params=_params("parallel"), name="mix_out",
    )(x1, *attn_o, *attn_lse, o_fwd, o_bwd, qg, gates, expand, wpa, wph, wout, hnorm, post)


def _interleave_gate_up(w_in):
    d = w_in.shape[0]
    gate = w_in[:, :D_FF].reshape(d, D_FF // FFN_CHUNK, FFN_CHUNK)
    up = w_in[:, D_FF:].reshape(d, D_FF // FFN_CHUNK, FFN_CHUNK)
    return jnp.stack([gate, up], axis=2).reshape(d, 2 * D_FF).astype(BF16)


def kernel(x, ffn1_pre_norm, w_ffn1_in, w_ffn1_out, ffn1_post_norm, mix_pre_norm, w_in, rel_bias, hgrn_lb,
           hgrn_norm, w_proj_attn, w_proj_hgrn, w_out, mix_post_norm, ffn2_pre_norm, w_ffn2_in, w_ffn2_out,
           ffn2_post_norm):
    batch, seq, d = x.shape
    assert d == D_MODEL and batch == 1 and w_in.shape[0] == 1
    row = lambda g: g.reshape(1, -1).astype(F32)

    lb_all = jnp.cumsum(jax.nn.softmax(hgrn_lb.astype(F32), axis=1), axis=1)
    lb = jnp.concatenate([lb_all[0, 0], lb_all[1, 0]]).reshape(1, 2 * HGRN_WIDTH)

    w = w_in[0]
    a3, hw = 3 * ATTN_WIDTH, HGRN_WIDTH
    seg = lambda k: w[:, a3 + k * hw:a3 + (k + 1) * hw]
    w_cast = jnp.concatenate([w[:, :a3], seg(3)], axis=1).astype(BF16)
    w_silu = jnp.concatenate([seg(0), seg(4)], axis=1).astype(BF16)
    w_logf = jnp.concatenate([seg(1), seg(2)], axis=1).astype(BF16)
    w_gate = w[:, a3 + 5 * hw:].astype(BF16)

    x1, h = _ffn(x.reshape(seq, d), row(ffn1_pre_norm), _interleave_gate_up(w_ffn1_in[0]),
                 w_ffn1_out[0].astype(BF16), row(ffn1_post_norm), row(mix_pre_norm))

    zc = _proj(h, w_cast, "cast", BF16)
    qg = _proj(h, w_silu, "silu", BF16)
    logf = _proj(h, w_logf, "logf", F32, lb=lb)
    gates = _proj(h, w_gate, "sigmoid", BF16)

    attn_o, attn_lse = [], []
    for g, (_, dilation) in enumerate(ATTN_GROUPS):
        o_g, lse_g = _attn(zc, _band_bias(rel_bias, g, dilation), g, dilation, seq)
        attn_o.append(o_g)
        attn_lse.append(lse_g)

    o_fwd = _scan(qg, zc, logf, _chunk_tri(False), seq, reverse=False)
    o_bwd = _scan(qg, zc, logf, _chunk_tri(True), seq, reverse=True)

    head_of_lane = np.arange(GROUP_WIDTH) // HEAD_DIM
    expand = jnp.asarray(np.arange(LSE_LANES)[:, None] == head_of_lane[None, :], dtype=BF16)
    x2 = _mix_out(x1, attn_o, attn_lse, o_fwd, o_bwd, qg, gates, expand,
                  w_proj_attn[0].astype(BF16), w_proj_hgrn[0].astype(BF16), w_out[0].astype(BF16),
                  row(hgrn_norm), row(mix_post_norm))

    (x3,) = _ffn(x2, row(ffn2_pre_norm), _interleave_gate_up(w_ffn2_in[0]), w_ffn2_out[0].astype(BF16),
                 row(ffn2_post_norm), None)
    return x3.reshape(batch, seq, d)
```
